```python
import jax, jax.numpy as jnp
from jax import lax
import numpy as np

D_MODEL = 4096
BATCH = 4
SEQ = 4096
DEPTH = 1

CHUNK = 64
MIX_WIDTH = D_MODEL
HGRN_WIDTH = MIX_WIDTH // 2
ATTN_WIDTH = MIX_WIDTH - HGRN_WIDTH
HGRN_HEAD_DIM = 128
HGRN_HEADS = HGRN_WIDTH // HGRN_HEAD_DIM
ATTN_HEAD_DIM = 128
ATTN_HEADS = ATTN_WIDTH // ATTN_HEAD_DIM
LEFT_CHUNKS = 8
BAND = (LEFT_CHUNKS + 1) * CHUNK
REL_MAX = 256
REL_SIZE = (CHUNK - 1) + REL_MAX + 1
PROJ_WIDTH = 4 * HGRN_WIDTH + 3 * ATTN_WIDTH
N_EXPERTS = 32
TOP_K = 4
D_EXPERT = (3 * D_MODEL) // 8
SWIGLU_LIMIT = 7.0
SWIGLU_ALPHA = 1.702
MOE_BLOCK = 128
RMS_EPS = 1e-6

kernel_name = "hybrid_hgrn2_chunkattn_moe_block"


def rms_norm(x, gain):
    xf = x.astype(jnp.float32)
    y = xf * lax.rsqrt(jnp.mean(xf * xf, axis=-1, keepdims=True) + RMS_EPS)
    return (y * gain.astype(jnp.float32)).astype(x.dtype)


def head_rms_norm(o, gain, n_heads, head_dim):
    y = o * lax.rsqrt(jnp.mean(o * o, axis=-1, keepdims=True) + RMS_EPS)
    return y * gain.astype(jnp.float32).reshape(n_heads, head_dim)


def chunkwise_gated_recurrence(q, k, v, log_f):
    B, S, H, dk = q.shape
    dv = v.shape[-1]
    n_chunks = S // CHUNK

    def to_chunks(t):
        return t.reshape(B, n_chunks, CHUNK, H, t.shape[-1]).transpose(1, 0, 3, 2, 4)

    qc, kc, vc, gc = to_chunks(q), to_chunks(k), to_chunks(v), to_chunks(log_f)
    causal = jnp.tril(jnp.ones((CHUNK, CHUNK), dtype=bool))[:, :, None]

    def step(state, inp):
        qb, kb, vb, gb = inp
        b = jnp.cumsum(gb, axis=-2)
        diff = b[..., :, None, :] - b[..., None, :, :]
        decay = jnp.exp(jnp.where(causal, diff, -jnp.inf))
        scores = jnp.einsum('bhtd,bhsd,bhtsd->bhts', qb, kb, decay)
        o_intra = jnp.einsum('bhts,bhsv->bhtv', scores, vb)
        o_inter = jnp.einsum('bhtd,bhdv->bhtv', qb * jnp.exp(b), state)
        b_last = b[..., -1:, :]
        k_dec = kb * jnp.exp(b_last - b)
        new_state = state * jnp.exp(b_last[..., 0, :])[..., None] + jnp.einsum('bhsd,bhsv->bhdv', k_dec, vb)
        return new_state, o_intra + o_inter

    state0 = jnp.zeros((B, H, dk, dv), jnp.float32)
    _, o = lax.scan(step, state0, (qc, kc, vc, gc))
    return o.transpose(1, 0, 3, 2, 4).reshape(B, S, H, dv)


def hgrn2_mixer(q_raw, f_raw, i_raw, g_raw, lb, out_gain):
    B, S, _ = q_raw.shape
    f32 = jnp.float32
    z = f_raw.astype(f32)
    lbf = lb.astype(f32)
    log_f = jnp.log(lbf + (1.0 - lbf) * jax.nn.sigmoid(z))
    k = (1.0 - lbf) * jax.nn.sigmoid(-z)
    q = jax.nn.silu(q_raw.astype(f32))
    v = i_raw.astype(f32)
    split = lambda t: t.reshape(B, S, HGRN_HEADS, HGRN_HEAD_DIM)
    o = chunkwise_gated_recurrence(split(q), split(k), split(v), split(log_f))
    o = head_rms_norm(o, out_gain, HGRN_HEADS, HGRN_HEAD_DIM).reshape(B, S, HGRN_WIDTH)
    return o * jax.nn.silu(g_raw.astype(f32))


def chunk_band_attention(q, k, v, rel_bias, out_gain):
    B, S, _ = q.shape
    n_chunks = S // CHUNK
    H, dh = ATTN_HEADS, ATTN_HEAD_DIM
    f32 = jnp.float32
    pad = LEFT_CHUNKS * CHUNK
    qh = q.astype(f32).reshape(B, n_chunks, CHUNK, H, dh).transpose(1, 0, 2, 3, 4)
    kp = jnp.pad(k.astype(f32).reshape(B, S, H, dh), ((0, 0), (pad, 0), (0, 0), (0, 0)))
    vp = jnp.pad(v.astype(f32).reshape(B, S, H, dh), ((0, 0), (pad, 0), (0, 0), (0, 0)))
    band_off = jnp.arange(BAND)
    rel = jnp.arange(CHUNK)[:, None] + pad - band_off[None, :]
    rel_idx = jnp.clip(rel, -(CHUNK - 1), REL_MAX) + (CHUNK - 1)
    bias = rel_bias.astype(f32)[:, rel_idx]
    scale = ATTN_HEAD_DIM ** -0.5

    def one_chunk(args):
        qb, c = args
        start = c * CHUNK
        kb = lax.dynamic_slice_in_dim(kp, start, BAND, axis=1)
        vb = lax.dynamic_slice_in_dim(vp, start, BAND, axis=1)
        s = jnp.einsum('bqhd,bkhd->bhqk', qb, kb) * scale + bias
        key_valid = (start + band_off) >= pad
        s = jnp.where(key_valid, s, -jnp.inf)
        p = jax.nn.softmax(s, axis=-1)
        return jnp.einsum('bhqk,bkhd->bqhd', p, vb)

    o = lax.map(one_chunk, (qh, jnp.arange(n_chunks)))
    o = o.transpose(1, 0, 2, 3, 4).reshape(B, S, H, dh)
    return head_rms_norm(o, out_gain, H, dh).reshape(B, S, ATTN_WIDTH)


def moe_ffn(xn, w_router, b_router, w_gate_up, b_gate_up, w_down, b_down):
    B, S, D = xn.shape
    T = B * S
    A = T * TOP_K
    f32 = jnp.float32
    xt = xn.reshape(T, D)
    logits = (xt @ w_router + b_router).astype(f32)
    top_vals, top_idx = lax.top_k(logits, TOP_K)
    gates = jax.nn.softmax(top_vals, axis=-1)
    e_flat = top_idx.reshape(A)
    tok_flat = jnp.arange(A, dtype=jnp.int32) // TOP_K
    g_flat = gates.reshape(A)
    order = jnp.argsort(e_flat)
    e_sorted = e_flat[order]
    counts = jnp.bincount(e_flat, length=N_EXPERTS)
    padded = ((counts + MOE_BLOCK - 1) // MOE_BLOCK) * MOE_BLOCK
    start = jnp.cumsum(counts) - counts
    pend = jnp.cumsum(padded)
    pstart = pend - padded
    dest = pstart[e_sorted] + (jnp.arange(A) - start[e_sorted])
    P = A + N_EXPERTS * MOE_BLOCK
    n_blocks = P // MOE_BLOCK
    row_tok = jnp.zeros((P,), jnp.int32).at[dest].set(tok_flat[order])
    row_gate = jnp.zeros((P,), f32).at[dest].set(g_flat[order])
    block_exp = jnp.minimum(jnp.searchsorted(pend, jnp.arange(n_blocks) * MOE_BLOCK, side='right'), N_EXPERTS - 1)

    def expert_block(acc, blk):
        rows, gate, e = blk
        xb = xt[rows]
        hgu = xb @ w_gate_up[e] + b_gate_up[e]
        glu = jnp.minimum(hgu[:, 0::2], SWIGLU_LIMIT)
        lin = jnp.clip(hgu[:, 1::2], -SWIGLU_LIMIT, SWIGLU_LIMIT)
        act = glu * jax.nn.sigmoid(SWIGLU_ALPHA * glu) * (lin + 1.0)
        out = act @ w_down[e] + b_down[e]
        return acc.at[rows].add(out.astype(f32) * gate[:, None]), None

    y0 = jnp.zeros((T, D), f32)
    y, _ = lax.scan(expert_block, y0, (row_tok.reshape(n_blocks, MOE_BLOCK), row_gate.reshape(n_blocks, MOE_BLOCK), block_exp))
    return y.reshape(B, S, D).astype(xn.dtype)


def setup_inputs(seed: int = 0) -> dict:
    key = jax.random.key(seed)
    ks = jax.random.split(key, 16)
    f32 = jnp.float32
    nrm = lambda k, shape: jax.random.normal(k, shape, f32)
    return {
        "x": nrm(ks[0], (BATCH, SEQ, D_MODEL)),
        "norm_mix_gain": 1.0 + 0.02 * nrm(ks[1], (DEPTH, D_MODEL)),
        "w_in": nrm(ks[2], (DEPTH, D_MODEL, PROJ_WIDTH)) * D_MODEL ** -0.5,
        "hgrn_lb_logits": 1.0 + 0.5 * nrm(ks[3], (DEPTH + 1, HGRN_WIDTH)),
        "hgrn_out_gain": 1.0 + 0.02 * nrm(ks[4], (DEPTH, HGRN_WIDTH)),
        "rel_bias": 0.2 * nrm(ks[5], (DEPTH, ATTN_HEADS, REL_SIZE)),
        "attn_out_gain": 1.0 + 0.02 * nrm(ks[6], (DEPTH, ATTN_WIDTH)),
        "w_out": nrm(ks[7], (DEPTH, MIX_WIDTH, D_MODEL)) * MIX_WIDTH ** -0.5,
        "norm_ffn_gain": 1.0 + 0.02 * nrm(ks[8], (DEPTH, D_MODEL)),
        "w_router": nrm(ks[9], (DEPTH, D_MODEL, N_EXPERTS)) * D_MODEL ** -0.5,
        "b_router": 0.01 * nrm(ks[10], (DEPTH, N_EXPERTS)),
        "w_gate_up": nrm(ks[11], (DEPTH, N_EXPERTS, D_MODEL, 2 * D_EXPERT)) * D_MODEL ** -0.5,
        "b_gate_up": 0.02 * nrm(ks[12], (DEPTH, N_EXPERTS, 2 * D_EXPERT)),
        "w_down": nrm(ks[13], (DEPTH, N_EXPERTS, D_EXPERT, D_MODEL)) * D_EXPERT ** -0.5,
        "b_down": 0.02 * nrm(ks[14], (DEPTH, N_EXPERTS, D_MODEL)),
        "final_gain": 1.0 + 0.02 * nrm(ks[15], (D_MODEL,)),
    }


def reference(x, norm_mix_gain, w_in, hgrn_lb_logits, hgrn_out_gain, rel_bias, attn_out_gain, w_out,
              norm_ffn_gain, w_router, b_router, w_gate_up, b_gate_up, w_down, b_down, final_gain):
    lower_bounds = jnp.cumsum(jax.nn.softmax(hgrn_lb_logits.astype(jnp.float32), axis=0), axis=0)
    offs = np.cumsum([0] + [HGRN_WIDTH] * 4 + [ATTN_WIDTH] * 3)
    h = x
    for l in range(DEPTH):
        xn = rms_norm(h, norm_mix_gain[l])
        proj = xn @ w_in[l]
        hq, hf, hi, hg, aq, ak, av = [proj[..., offs[j]:offs[j + 1]] for j in range(7)]
        o_hgrn = hgrn2_mixer(hq, hf, hi, hg, lower_bounds[l], hgrn_out_gain[l])
        o_attn = chunk_band_attention(aq, ak, av, rel_bias[l], attn_out_gain[l])
        mixed = jnp.concatenate([o_hgrn, o_attn], axis=-1).astype(h.dtype) @ w_out[l]
        h = h + mixed
        xn2 = rms_norm(h, norm_ffn_gain[l])
        h = h + moe_ffn(xn2, w_router[l], b_router[l], w_gate_up[l], b_gate_up[l], w_down[l], b_down[l])
    return rms_norm(h, final_gain)
```

```python
import functools

import jax
import jax.numpy as jnp
from jax import lax
from jax.experimental import pallas as pl
from jax.experimental.pallas import tpu as pltpu

F32 = jnp.float32
BF16 = jnp.bfloat16
I32 = jnp.int32

D_MODEL = 4096
HEAD_DIM = 128
N_HEADS = 16
GROUP_W = N_HEADS * HEAD_DIM
PROJ_W = 7 * GROUP_W
CHUNK = 64
SUB = 16
LEFT_CHUNKS = 8
REL_MAX = 256
N_EXPERTS = 32
TOP_K = 4
D_EXPERT = 1536
SWIGLU_LIMIT = 7.0
SWIGLU_ALPHA = 1.702
RMS_EPS = 1e-6
NEG = -1e30

VMEM_LIMIT = 60 * 1024 * 1024

TM_IN, TN_IN = 512, 1024
HGRN_ROWS = 512
ATT_Q = 256
ATT_K = ATT_Q + LEFT_CHUNKS * CHUNK
TM_OUT, TN_OUT = 512, 1024
TM_NR = 256
TM_MOE, TF_MOE = 512, 512
TG = 128


def _cparams(sem):
    return pltpu.CompilerParams(dimension_semantics=sem, vmem_limit_bytes=VMEM_LIMIT)


def _in_proj_kernel(x_ref, g_ref, w_ref, o_ref, xn_ref):
    @pl.when(pl.program_id(1) == 0)
    def _():
        x = x_ref[...]
        ms = jnp.mean(x * x, axis=-1, keepdims=True)
        xn_ref[...] = (x * lax.rsqrt(ms + RMS_EPS) * g_ref[...]).astype(BF16)

    o_ref[...] = jnp.dot(xn_ref[...], w_ref[...], preferred_element_type=F32).astype(o_ref.dtype)


def _in_proj(x2, gain, w_bf16):
    t, d = x2.shape
    n = w_bf16.shape[1]
    return pl.pallas_call(
        _in_proj_kernel,
        grid=(t // TM_IN, n // TN_IN),
        in_specs=[
            pl.BlockSpec((TM_IN, d), lambda i, j: (i, 0)),
            pl.BlockSpec((1, d), lambda i, j: (0, 0)),
            pl.BlockSpec((d, TN_IN), lambda i, j: (0, j)),
        ],
        out_specs=pl.BlockSpec((TM_IN, TN_IN), lambda i, j: (i, j)),
        out_shape=jax.ShapeDtypeStruct((t, n), BF16),
        scratch_shapes=[pltpu.VMEM((TM_IN, d), BF16)],
        compiler_params=_cparams(("parallel", "arbitrary")),
        name="in_proj",
    )(x2, gain.reshape(1, d), w_bf16)


def _sigmoid(x):
    return 1.0 / (1.0 + jnp.exp(-x))


def _hgrn_kernel(q_ref, f_ref, i_ref, g_ref, lb_ref, gain_ref, o_ref, st_ref, qs_ref, ks_ref, lf_ref):
    @pl.when(pl.program_id(2) == 0)
    def _():
        st_ref[...] = jnp.zeros_like(st_ref)

    lb = lb_ref[...]
    z = f_ref[...].astype(F32)
    lf_ref[...] = jnp.log(lb + (1.0 - lb) * _sigmoid(z))
    ks_ref[...] = (1.0 - lb) * _sigmoid(-z)
    qr = q_ref[...].astype(F32)
    qs_ref[...] = qr * _sigmoid(qr)

    ti = lax.broadcasted_iota(I32, (2 * CHUNK, CHUNK), 0)
    si = lax.broadcasted_iota(I32, (2 * CHUNK, CHUNK), 1)
    t_lo = ti % CHUNK
    blk_start = (t_lo // SUB) * SUB
    in_blk = (ti < CHUNK) & (si >= blk_start) & (si <= t_lo)
    before = (ti >= CHUNK) & (si < blk_start)
    cum_mat = jnp.where(in_blk | before, 1.0, 0.0).astype(BF16)

    row16 = lax.broadcasted_iota(I32, (SUB, HEAD_DIM), 0)
    row16c = lax.broadcasted_iota(I32, (SUB, CHUNK), 0)
    lane64 = lax.broadcasted_iota(I32, (SUB, CHUNK), 1)
    gain = gain_ref[...]
    nt = (((1,), (1,)), ((), ()))
    tn = (((0,), (0,)), ((), ()))
    n_sub = CHUNK // SUB

    def chunk_body(c, st):
        r = pl.multiple_of(c * CHUNK, CHUNK)
        q = qs_ref[pl.ds(r, CHUNK), :]
        k = ks_ref[pl.ds(r, CHUNK), :]
        lf = lf_ref[pl.ds(r, CHUNK), :]
        v = i_ref[pl.ds(r, CHUNK), :]

        lf_hi = lf.astype(BF16)
        lf_lo = (lf - lf_hi.astype(F32)).astype(BF16)
        cs = jnp.dot(cum_mat, jnp.concatenate([lf_hi, lf_lo], axis=1), preferred_element_type=F32)
        w = cs[:CHUNK, :HEAD_DIM] + cs[:CHUNK, HEAD_DIM:]
        m = cs[CHUNK:, :HEAD_DIM] + cs[CHUNK:, HEAD_DIM:]
        b = w + m
        qh = q * jnp.exp(w)
        qt = qh * jnp.exp(m)
        b_last = b[CHUNK - 1:CHUNK, :]
        kdec = k * jnp.exp(b_last - b)

        qh16 = qh.astype(BF16)
        score_rows = []
        for i in range(n_sub):
            r0 = i * SUB
            qb = q[r0:r0 + SUB, :]
            kb = k[r0:r0 + SUB, :]
            wb = w[r0:r0 + SUB, :]
            sc = jnp.zeros((SUB, CHUNK), F32)
            for s in range(SUB):
                e = jnp.exp(jnp.minimum(wb - wb[s:s + 1, :], 0.0))
                col = jnp.sum(qb * (kb[s:s + 1, :] * e), axis=-1, keepdims=True)
                col = jnp.where(row16c >= s, col, 0.0)
                sc = jnp.where(lane64 == (r0 + s), col, sc)
            if i > 0:
                m_i = m[r0:r0 + 1, :]
                kt = k[:r0, :] * jnp.exp(m_i - b[:r0, :])
                kt = jnp.concatenate([kt, jnp.zeros((CHUNK - r0, HEAD_DIM), F32)], axis=0).astype(BF16)
                sc = sc + lax.dot_general(qh16[r0:r0 + SUB, :], kt, nt, preferred_element_type=F32)
            score_rows.append(sc)
        scores = jnp.concatenate(score_rows, axis=0).astype(BF16)

        o = jnp.dot(scores, v, preferred_element_type=F32)
        o = o + lax.dot_general(qt.astype(BF16), st.astype(BF16), nt, preferred_element_type=F32)
        st_new = st * jnp.exp(b_last) + lax.dot_general(v, kdec.astype(BF16), tn, preferred_element_type=F32)

        y = o * lax.rsqrt(jnp.mean(o * o, axis=-1, keepdims=True) + RMS_EPS) * gain
        gr = g_ref[pl.ds(r, CHUNK), :].astype(F32)
        o_ref[pl.ds(r, CHUNK), :] = (y * (gr * _sigmoid(gr))).astype(o_ref.dtype)
        return st_new

    del row16
    st_ref[...] = lax.fori_loop(0, HGRN_ROWS // CHUNK, chunk_body, st_ref[...])


def _hgrn(proj3, lb, gain):
    b, s, _ = proj3.shape

    def col(group):
        return pl.BlockSpec((None, HGRN_ROWS, HEAD_DIM), lambda bi, h, si: (bi, si, group * N_HEADS + h))

    vec = pl.BlockSpec((1, HEAD_DIM), lambda bi, h, si: (0, h))
    return pl.pallas_call(
        _hgrn_kernel,
        grid=(b, N_HEADS, s // HGRN_ROWS),
        in_specs=[col(0), col(1), col(2), col(3), vec, vec],
        out_specs=pl.BlockSpec((None, HGRN_ROWS, HEAD_DIM), lambda bi, h, si: (bi, si, h)),
        out_shape=jax.ShapeDtypeStruct((b, s, GROUP_W), BF16),
        scratch_shapes=[
            pltpu.VMEM((HEAD_DIM, HEAD_DIM), F32),
            pltpu.VMEM((HGRN_ROWS, HEAD_DIM), F32),
            pltpu.VMEM((HGRN_ROWS, HEAD_DIM), F32),
            pltpu.VMEM((HGRN_ROWS, HEAD_DIM), F32),
        ],
        compiler_params=_cparams(("parallel", "parallel", "arbitrary")),
        name="hgrn",
    )(proj3, proj3, proj3, proj3, lb.reshape(1, GROUP_W), gain.reshape(1, GROUP_W))


def _attn_kernel(q_ref, k0_ref, k1_ref, k2_ref, v0_ref, v1_ref, v2_ref, bias_ref, gain_ref, o_ref):
    i = pl.program_id(2)
    q = q_ref[...]
    k = jnp.concatenate([k0_ref[...], k1_ref[...], k2_ref[...]], axis=0)
    v = jnp.concatenate([v0_ref[...], v1_ref[...], v2_ref[...]], axis=0)
    s = lax.dot_general(q, k, (((1,), (1,)), ((), ())), preferred_element_type=F32)
    s = s * (HEAD_DIM ** -0.5) + bias_ref[...]
    kpos = (i - 2) * ATT_Q + lax.broadcasted_iota(I32, (1, 3 * ATT_Q), 1)
    s = jnp.where(kpos >= 0, s, NEG)
    mx = jnp.max(s, axis=-1, keepdims=True)
    p = jnp.exp(s - mx)
    l = jnp.sum(p, axis=-1, keepdims=True)
    o = jnp.dot(p.astype(BF16), v, preferred_element_type=F32) / l
    y = o * lax.rsqrt(jnp.mean(o * o, axis=-1, keepdims=True) + RMS_EPS) * gain_ref[...]
    o_ref[...] = y.astype(o_ref.dtype)


def _attn_bias_blocks(rel_bias):
    band = (LEFT_CHUNKS + 1) * CHUNK
    rel = jnp.arange(CHUNK)[:, None] + LEFT_CHUNKS * CHUNK - jnp.arange(band)[None, :]
    rel_idx = jnp.clip(rel, -(CHUNK - 1), REL_MAX) + (CHUNK - 1)
    bias = rel_bias.astype(F32)[:, rel_idx]
    n_q = ATT_Q // CHUNK
    rows = []
    for c in range(n_q):
        left = c * CHUNK
        right = 3 * ATT_Q - band - left
        rows.append(jnp.pad(bias, ((0, 0), (0, 0), (left, right)), constant_values=NEG))
    return jnp.concatenate(rows, axis=1)


def _attn(proj3, bias_blk, gain):
    b, s, _ = proj3.shape
    qc, kc, vc = 4 * N_HEADS, 5 * N_HEADS, 6 * N_HEADS

    def kv(base, back):
        return pl.BlockSpec((None, ATT_Q, HEAD_DIM),
                            lambda h, bi, i: (bi, jnp.maximum(i - back, 0), base + h))

    return pl.pallas_call(
        _attn_kernel,
        grid=(N_HEADS, b, s // ATT_Q),
        in_specs=[
            pl.BlockSpec((None, ATT_Q, HEAD_DIM), lambda h, bi, i: (bi, i, qc + h)),
            kv(kc, 2), kv(kc, 1), kv(kc, 0),
            kv(vc, 2), kv(vc, 1), kv(vc, 0),
            pl.BlockSpec((None, ATT_Q, 3 * ATT_Q), lambda h, bi, i: (h, 0, 0)),
            pl.BlockSpec((1, HEAD_DIM), lambda h, bi, i: (0, h)),
        ],
        out_specs=pl.BlockSpec((None, ATT_Q, HEAD_DIM), lambda h, bi, i: (bi, i, h)),
        out_shape=jax.ShapeDtypeStruct((b, s, GROUP_W), BF16),
        compiler_params=_cparams(("parallel", "parallel", "arbitrary")),
        name="attn",
    )(proj3, proj3, proj3, proj3, proj3, proj3, proj3, bias_blk, gain.reshape(1, GROUP_W))


def _out_proj_kernel(oh_ref, oa_ref, wh_ref, wa_ref, x_ref, h_ref):
    acc = jnp.dot(oh_ref[...], wh_ref[...], preferred_element_type=F32)
    acc = acc + jnp.dot(oa_ref[...], wa_ref[...], preferred_element_type=F32)
    h_ref[...] = x_ref[...] + acc


def _out_proj(oh, oa, wh, wa, x2):
    t, d = x2.shape
    return pl.pallas_call(
        _out_proj_kernel,
        grid=(t // TM_OUT, d // TN_OUT),
        in_specs=[
            pl.BlockSpec((TM_OUT, GROUP_W), lambda i, j: (i, 0)),
            pl.BlockSpec((TM_OUT, GROUP_W), lambda i, j: (i, 0)),
            pl.BlockSpec((GROUP_W, TN_OUT), lambda i, j: (0, j)),
            pl.BlockSpec((GROUP_W, TN_OUT), lambda i, j: (0, j)),
            pl.BlockSpec((TM_OUT, TN_OUT), lambda i, j: (i, j)),
        ],
        out_specs=pl.BlockSpec((TM_OUT, TN_OUT), lambda i, j: (i, j)),
        out_shape=jax.ShapeDtypeStruct((t, d), F32),
        compiler_params=_cparams(("parallel", "arbitrary")),
        name="out_proj",
    )(oh, oa, wh, wa, x2)


def _bf16_bits(x):
    return lax.bitcast_convert_type(x.astype(BF16).astype(F32), I32)


def _norm_router_kernel(h_ref, g_ref, wr_hi_ref, wr_lo_ref, br_ref, xp_ref, lg_ref):
    h = h_ref[...]
    ms = jnp.mean(h * h, axis=-1, keepdims=True)
    xn = h * lax.rsqrt(ms + RMS_EPS) * g_ref[...]
    half = D_MODEL // 2
    lo = lax.shift_right_logical(_bf16_bits(xn[:, :half]), 16)
    hi = _bf16_bits(xn[:, half:]) & jnp.int32(-65536)
    xp_ref[...] = lo | hi
    x_hi = xn.astype(BF16)
    x_lo = (xn - x_hi.astype(F32)).astype(BF16)
    lg = jnp.dot(x_hi, wr_hi_ref[...], preferred_element_type=F32)
    lg = lg + jnp.dot(x_lo, wr_hi_ref[...], preferred_element_type=F32)
    lg = lg + jnp.dot(x_hi, wr_lo_ref[...], preferred_element_type=F32)
    lg_ref[...] = lg + br_ref[...]


def _norm_router(h, gain, w_router, b_router):
    t, d = h.shape
    wr_hi = w_router.astype(BF16)
    wr_lo = (w_router - wr_hi.astype(F32)).astype(BF16)
    return pl.pallas_call(
        _norm_router_kernel,
        grid=(t // TM_NR,),
        in_specs=[
            pl.BlockSpec((TM_NR, d), lambda i: (i, 0)),
            pl.BlockSpec((1, d), lambda i: (0, 0)),
            pl.BlockSpec((d, N_EXPERTS), lambda i: (0, 0)),
            pl.BlockSpec((d, N_EXPERTS), lambda i: (0, 0)),
            pl.BlockSpec((1, N_EXPERTS), lambda i: (0, 0)),
        ],
        out_specs=[
            pl.BlockSpec((TM_NR, d // 2), lambda i: (i, 0)),
            pl.BlockSpec((TM_NR, N_EXPERTS), lambda i: (i, 0)),
        ],
        out_shape=[
            jax.ShapeDtypeStruct((t, d // 2), I32),
            jax.ShapeDtypeStruct((t, N_EXPERTS), F32),
        ],
        compiler_params=_cparams(("parallel",)),
        name="norm_router",
    )(h, gain.reshape(1, d), wr_hi, wr_lo, b_router.reshape(1, N_EXPERTS))


def _row_copy(src_hbm, dst_vmem, src_row, dst_row, sem):
    return pltpu.make_async_copy(src_hbm.at[pl.ds(src_row, 1)], dst_vmem.at[pl.ds(dst_row, 1)], sem)


def _moe_kernel(te_ref, nu_ref, ids_ref, xp_hbm, wg_ref, wl_ref, bg_ref, bl_ref, wd_ref, bd_ref,
                y_ref, xg_ref, xb_ref, sem):
    t = pl.program_id(0)
    f = pl.program_id(1)
    valid = t < nu_ref[0]
    half = D_MODEL // 2

    @pl.when(valid & (f == 0))
    def _():
        def start(r, c):
            _row_copy(xp_hbm, xg_ref, ids_ref[0, r], r, sem).start()
            return c

        lax.fori_loop(0, TM_MOE, start, 0)

        def wait(r, c):
            _row_copy(xp_hbm, xg_ref, 0, r, sem).wait()
            return c

        lax.fori_loop(0, TM_MOE, wait, 0)
        xg = xg_ref[...]
        xb_ref[:, :half] = lax.bitcast_convert_type(lax.shift_left(xg, 16), F32).astype(BF16)
        xb_ref[:, half:] = lax.bitcast_convert_type(xg & jnp.int32(-65536), F32).astype(BF16)

    @pl.when(valid)
    def _():
        x = xb_ref[...]
        glu = jnp.dot(x, wg_ref[...], preferred_element_type=F32) + bg_ref[...]
        lin = jnp.dot(x, wl_ref[...], preferred_element_type=F32) + bl_ref[...]
        glu = jnp.minimum(glu, SWIGLU_LIMIT)
        lin = jnp.clip(lin, -SWIGLU_LIMIT, SWIGLU_LIMIT)
        act = glu * _sigmoid(SWIGLU_ALPHA * glu) * (lin + 1.0)
        part = jnp.dot(act.astype(BF16), wd_ref[...], preferred_element_type=F32)

        @pl.when(f == 0)
        def _():
            y_ref[...] = part + bd_ref[...]

        @pl.when(f != 0)
        def _():
            y_ref[...] += part

    @pl.when(jnp.logical_not(valid) & (f == 0))
    def _():
        y_ref[...] = jnp.zeros_like(y_ref)


def _moe_experts(tile_expert, n_used, ids3, xp, wg, wl, bg, bl, wd, bd):
    n_max = ids3.shape[0]
    nf = D_EXPERT // TF_MOE

    def fsel(t, f, nu):
        return jnp.where(t < nu[0], f, nf - 1)

    grid_spec = pltpu.PrefetchScalarGridSpec(
        num_scalar_prefetch=2,
        grid=(n_max, nf),
        in_specs=[
            pl.BlockSpec((None, 1, TM_MOE), lambda t, f, te, nu: (t, 0, 0), memory_space=pltpu.SMEM),
            pl.BlockSpec(memory_space=pl.ANY),
            pl.BlockSpec((None, D_MODEL, TF_MOE), lambda t, f, te, nu: (te[t], 0, fsel(t, f, nu))),
            pl.BlockSpec((None, D_MODEL, TF_MOE), lambda t, f, te, nu: (te[t], 0, fsel(t, f, nu))),
            pl.BlockSpec((None, 1, TF_MOE), lambda t, f, te, nu: (te[t], 0, fsel(t, f, nu))),
            pl.BlockSpec((None, 1, TF_MOE), lambda t, f, te, nu: (te[t], 0, fsel(t, f, nu))),
            pl.BlockSpec((None, TF_MOE, D_MODEL), lambda t, f, te, nu: (te[t], fsel(t, f, nu), 0)),
            pl.BlockSpec((None, 1, D_MODEL), lambda t, f, te, nu: (te[t], 0, 0)),
        ],
        out_specs=pl.BlockSpec((TM_MOE, D_MODEL), lambda t, f, te, nu: (t, 0)),
        scratch_shapes=[
            pltpu.VMEM((TM_MOE, D_MODEL // 2), I32),
            pltpu.VMEM((TM_MOE, D_MODEL), BF16),
            pltpu.SemaphoreType.DMA(()),
        ],
    )
    return pl.pallas_call(
        _moe_kernel,
        grid_spec=grid_spec,
        out_shape=jax.ShapeDtypeStruct((n_max * TM_MOE, D_MODEL), F32),
        compiler_params=_cparams(("arbitrary", "arbitrary")),
        name="moe_experts",
    )(tile_expert, n_used, ids3, xp, wg, wl, bg, bl, wd, bd)


def _combine_kernel(pos_ref, gates_ref, h_ref, fg_ref, ys_hbm, o_ref, buf_ref, sem):
    def start(r, c):
        for k in range(TOP_K):
            _row_copy(ys_hbm, buf_ref.at[k], pos_ref[0, r * TOP_K + k], r, sem).start()
        return c

    lax.fori_loop(0, TG, start, 0)

    def wait(r, c):
        for k in range(TOP_K):
            _row_copy(ys_hbm, buf_ref.at[k], 0, r, sem).wait()
        return c

    lax.fori_loop(0, TG, wait, 0)
    g = gates_ref[...]
    y = h_ref[...]
    for k in range(TOP_K):
        y = y + buf_ref[k] * g[:, k:k + 1]
    ms = jnp.mean(y * y, axis=-1, keepdims=True)
    o_ref[...] = y * lax.rsqrt(ms + RMS_EPS) * fg_ref[...]


def _combine(pos3, gates, h, final_gain, ys):
    t, d = h.shape
    return pl.pallas_call(
        _combine_kernel,
        grid=(t // TG,),
        in_specs=[
            pl.BlockSpec((None, 1, TG * TOP_K), lambda i: (i, 0, 0), memory_space=pltpu.SMEM),
            pl.BlockSpec((TG, TOP_K), lambda i: (i, 0)),
            pl.BlockSpec((TG, d), lambda i: (i, 0)),
            pl.BlockSpec((1, d), lambda i: (0, 0)),
            pl.BlockSpec(memory_space=pl.ANY),
        ],
        out_specs=pl.BlockSpec((TG, d), lambda i: (i, 0)),
        out_shape=jax.ShapeDtypeStruct((t, d), F32),
        scratch_shapes=[pltpu.VMEM((TOP_K, TG, d), F32), pltpu.SemaphoreType.DMA(())],
        compiler_params=_cparams(("arbitrary",)),
        name="combine",
    )(pos3, gates, h, final_gain.reshape(1, d), ys)


def _route(logits):
    t = logits.shape[0]
    a = t * TOP_K
    top_vals, top_idx = lax.top_k(logits, TOP_K)
    gates = jax.nn.softmax(top_vals, axis=-1)
    e_flat = top_idx.reshape(a).astype(I32)
    onehot = (e_flat[:, None] == jnp.arange(N_EXPERTS, dtype=I32)[None, :]).astype(I32)
    csum = jnp.cumsum(onehot, axis=0)
    rank = jnp.take_along_axis(csum, e_flat[:, None], axis=1)[:, 0] - 1
    counts = csum[-1]
    tiles_e = (counts + TM_MOE - 1) // TM_MOE
    tiles_end = jnp.cumsum(tiles_e)
    tiles_start = tiles_end - tiles_e
    dest = tiles_start[e_flat] * TM_MOE + rank
    n_max = a // TM_MOE + N_EXPERTS
    tok = jnp.arange(a, dtype=I32) // TOP_K
    row_tok = jnp.zeros((n_max * TM_MOE,), I32).at[dest].set(tok)
    n_used = tiles_end[-1].astype(I32)
    tile_ids = jnp.arange(n_max, dtype=I32)
    tile_expert = jnp.searchsorted(tiles_end, jnp.minimum(tile_ids, n_used - 1), side="right")
    tile_expert = jnp.minimum(tile_expert, N_EXPERTS - 1).astype(I32)
    return gates, dest.astype(I32), row_tok, tile_expert, n_used.reshape(1)


def kernel(x, norm_mix_gain, w_in, hgrn_lb_logits, hgrn_out_gain, rel_bias, attn_out_gain, w_out,
           norm_ffn_gain, w_router, b_router, w_gate_up, b_gate_up, w_down, b_down, final_gain):
    b, s, d = x.shape
    t = b * s
    x2 = x.reshape(t, d)

    w_in_b = w_in[0].astype(BF16)
    w_out_b = w_out[0].astype(BF16)
    wg = w_gate_up[0][:, :, 0::2].astype(BF16)
    wl = w_gate_up[0][:, :, 1::2].astype(BF16)
    bg = b_gate_up[0][:, None, 0::2]
    bl = b_gate_up[0][:, None, 1::2]
    wd = w_down[0].astype(BF16)
    bd = b_down[0][:, None, :]
    lower_bound = jax.nn.softmax(hgrn_lb_logits.astype(F32), axis=0)[0]
    bias_blk = _attn_bias_blocks(rel_bias[0])

    proj = _in_proj(x2, norm_mix_gain[0], w_in_b)
    proj3 = proj.reshape(b, s, PROJ_W)
    o_h = _hgrn(proj3, lower_bound, hgrn_out_gain[0]).reshape(t, GROUP_W)
    o_a = _attn(proj3, bias_blk, attn_out_gain[0]).reshape(t, GROUP_W)
    h = _out_proj(o_h, o_a, w_out_b[:GROUP_W], w_out_b[GROUP_W:], x2)

    xp, logits = _norm_router(h, norm_ffn_gain[0], w_router[0], b_router[0])
    gates, dest, row_tok, tile_expert, n_used = _route(logits)
    ids3 = row_tok.reshape(-1, 1, TM_MOE)
    ys = _moe_experts(tile_expert, n_used, ids3, xp, wg, wl, bg, bl, wd, bd)

    pos3 = dest.reshape(t // TG, 1, TG * TOP_K)
    out = _combine(pos3, gates, h, final_gain, ys)
    return out.reshape(b, s, d)
```

```python
import jax
import jax.numpy as jnp
from jax import lax
from jax.experimental import pallas as pl
from jax.experimental.pallas import tpu as pltpu

F32 = jnp.float32
BF16 = jnp.bfloat16
I32 = jnp.int32

D_MODEL = 4096
HALF = D_MODEL // 2
HEAD_DIM = 128
N_HEADS = 16
GROUP_W = N_HEADS * HEAD_DIM
PROJ_W = 7 * GROUP_W
CHUNK = 64
SUB = 16
LEFT_CHUNKS = 8
REL_MAX = 256
N_EXPERTS = 32
TOP_K = 4
D_EXPERT = 1536
SWIGLU_LIMIT = 7.0
SWIGLU_ALPHA = 1.702
RMS_EPS = 1e-6
NEG = -1e30
LOG2E = 1.4426950408889634
HI_MASK = -65536

VMEM_LIMIT = 60 * 1024 * 1024
VMEM_LIMIT_MOE = 62 * 1024 * 1024
MXU_COLS = 256

TM_IN, TN_IN = 512, 1024
HGRN_ROWS = 512
HGRN_HB = 2
ATT_Q = 256
ATT_HB = 2
TM_OUT, TN_OUT = 512, 1024
TM_NR = 256
TM_MOE, TF_MOE = 512, 512
NF_MOE = D_EXPERT // TF_MOE
TK_PREP = 1024
TG = 128
DMA_UNROLL = 8


def _cparams(sem, vmem_limit=VMEM_LIMIT):
    return pltpu.CompilerParams(dimension_semantics=sem, vmem_limit_bytes=vmem_limit)


def _sigmoid(x):
    return 0.5 * jnp.tanh(0.5 * x) + 0.5


def _pack_pairs(lo, hi):
    lo_b = lax.bitcast_convert_type(lo.astype(BF16).astype(F32), I32)
    hi_b = lax.bitcast_convert_type(hi.astype(BF16).astype(F32), I32)
    return lax.shift_right_logical(lo_b, 16) | (hi_b & jnp.int32(HI_MASK))


def _unpack_lo(p):
    return lax.bitcast_convert_type(lax.shift_left(p, 16), F32)


def _unpack_hi(p):
    return lax.bitcast_convert_type(p & jnp.int32(HI_MASK), F32)


def _gate_up_prep_kernel(w_ref, o_ref):
    ri = lax.broadcasted_iota(I32, (MXU_COLS, MXU_COLS), 0)
    ci = lax.broadcasted_iota(I32, (MXU_COLS, MXU_COLS), 1)
    half = MXU_COLS // 2
    src = jnp.where(ci < half, 2 * ci, 2 * (ci - half) + 1)
    perm = jnp.where(ri == src, 1.0, 0.0).astype(BF16)
    for g in range(2 * TF_MOE // MXU_COLS):
        cols = slice(g * MXU_COLS, (g + 1) * MXU_COLS)
        w = w_ref[:, cols].astype(BF16)
        o_ref[:, cols] = jnp.dot(w, perm, preferred_element_type=F32).astype(BF16)


def _gate_up_prep(w_gate_up):
    e, d, n2 = w_gate_up.shape
    return pl.pallas_call(
        _gate_up_prep_kernel,
        grid=(e, NF_MOE, d // TK_PREP),
        in_specs=[pl.BlockSpec((None, TK_PREP, 2 * TF_MOE), lambda ei, f, k: (ei, k, f))],
        out_specs=pl.BlockSpec((None, None, TK_PREP, 2 * TF_MOE), lambda ei, f, k: (ei, f, k, 0)),
        out_shape=jax.ShapeDtypeStruct((e, NF_MOE, d, 2 * TF_MOE), BF16),
        compiler_params=_cparams(("parallel", "parallel", "parallel")),
        name="gate_up_prep",
    )(w_gate_up)


def _gate_up_bias_prep(b_gate_up):
    e = b_gate_up.shape[0]
    half = MXU_COLS // 2
    b = b_gate_up.reshape(e, NF_MOE, 2 * TF_MOE // MXU_COLS, half, 2)
    return b.transpose(0, 1, 2, 4, 3).reshape(e, NF_MOE, 1, 2 * TF_MOE)


def _in_proj_kernel(x_ref, g_ref, w_ref, o_ref, xn_ref):
    @pl.when(pl.program_id(1) == 0)
    def _():
        x = x_ref[...]
        ms = jnp.mean(x * x, axis=-1, keepdims=True)
        xn_ref[...] = (x * lax.rsqrt(ms + RMS_EPS) * g_ref[...]).astype(BF16)

    o_ref[...] = jnp.dot(xn_ref[...], w_ref[...], preferred_element_type=F32).astype(o_ref.dtype)


def _in_proj(x2, gain, w_bf16):
    t, d = x2.shape
    n = w_bf16.shape[1]
    return pl.pallas_call(
        _in_proj_kernel,
        grid=(t // TM_IN, n // TN_IN),
        in_specs=[
            pl.BlockSpec((TM_IN, d), lambda i, j: (i, 0)),
            pl.BlockSpec((1, d), lambda i, j: (0, 0)),
            pl.BlockSpec((d, TN_IN), lambda i, j: (0, j)),
        ],
        out_specs=pl.BlockSpec((TM_IN, TN_IN), lambda i, j: (i, j)),
        out_shape=jax.ShapeDtypeStruct((t, n), BF16),
        scratch_shapes=[pltpu.VMEM((TM_IN, d), BF16)],
        compiler_params=_cparams(("parallel", "arbitrary")),
        name="in_proj",
    )(x2, gain.reshape(1, d), w_bf16)


def _hgrn_kernel(q_ref, f_ref, i_ref, g_ref, lb_ref, gain_ref, o_ref, st_ref, qs_ref, lk_ref, lf_ref):
    @pl.when(pl.program_id(2) == 0)
    def _():
        st_ref[...] = jnp.zeros_like(st_ref)

    lb = lb_ref[...]
    sig = _sigmoid(f_ref[...].astype(F32))
    lf_ref[...] = jnp.log2(lb + (1.0 - lb) * sig)
    lk_ref[...] = jnp.log2((1.0 - lb) * (1.0 - sig))
    qr = q_ref[...].astype(F32)
    qs_ref[...] = qr * _sigmoid(qr)

    ti = lax.broadcasted_iota(I32, (2 * CHUNK, CHUNK), 0)
    si = lax.broadcasted_iota(I32, (2 * CHUNK, CHUNK), 1)
    t_lo = ti % CHUNK
    blk_start = (t_lo // SUB) * SUB
    in_blk = (ti < CHUNK) & (si >= blk_start) & (si <= t_lo)
    before = (ti >= CHUNK) & (si < blk_start)
    cum_mat = jnp.where(in_blk | before, 1.0, 0.0).astype(BF16)

    row16c = lax.broadcasted_iota(I32, (SUB, CHUNK), 0)
    lane64 = lax.broadcasted_iota(I32, (SUB, CHUNK), 1)
    gain_all = gain_ref[...]
    nt = (((1,), (1,)), ((), ()))
    tn = (((0,), (0,)), ((), ()))
    n_sub = CHUNK // SUB

    def one_head(r, hh, st):
        cols = slice(hh * HEAD_DIM, (hh + 1) * HEAD_DIM)
        q = qs_ref[pl.ds(r, CHUNK), cols]
        lk = lk_ref[pl.ds(r, CHUNK), cols]
        lf = lf_ref[pl.ds(r, CHUNK), cols]
        v = i_ref[pl.ds(r, CHUNK), cols]

        lf_hi = lf.astype(BF16)
        lf_lo = (lf - lf_hi.astype(F32)).astype(BF16)
        cs = jnp.dot(cum_mat, jnp.concatenate([lf_hi, lf_lo], axis=1), preferred_element_type=F32)
        w = cs[:CHUNK, :HEAD_DIM] + cs[:CHUNK, HEAD_DIM:]
        m = cs[CHUNK:, :HEAD_DIM] + cs[CHUNK:, HEAD_DIM:]
        b = w + m
        qh = q * jnp.exp2(w)
        qt = qh * jnp.exp2(m)
        b_last = b[CHUNK - 1:CHUNK, :]
        kdec = jnp.exp2(b_last - b + lk)
        c = w - lk

        qh16 = qh.astype(BF16)
        score_rows = []
        for i in range(n_sub):
            r0 = i * SUB
            qb = q[r0:r0 + SUB, :]
            wb = w[r0:r0 + SUB, :]
            cb = c[r0:r0 + SUB, :]
            sc = jnp.zeros((SUB, CHUNK), F32)
            for s in range(SUB):
                col = jnp.sum(qb * jnp.exp2(wb - cb[s:s + 1, :]), axis=-1, keepdims=True)
                sc = jnp.where((lane64 == (r0 + s)) & (row16c >= s), col, sc)
            if i > 0:
                m_i = m[r0:r0 + 1, :]
                kt = jnp.exp2(m_i - b[:r0, :] + lk[:r0, :])
                kt = jnp.concatenate([kt, jnp.zeros((CHUNK - r0, HEAD_DIM), F32)], axis=0).astype(BF16)
                sc = sc + lax.dot_general(qh16[r0:r0 + SUB, :], kt, nt, preferred_element_type=F32)
            score_rows.append(sc)
        scores = jnp.concatenate(score_rows, axis=0).astype(BF16)

        o = jnp.dot(scores, v, preferred_element_type=F32)
        o = o + lax.dot_general(qt.astype(BF16), st.astype(BF16), nt, preferred_element_type=F32)
        st_new = st * jnp.exp2(b_last) + lax.dot_general(v, kdec.astype(BF16), tn, preferred_element_type=F32)

        y = o * lax.rsqrt(jnp.mean(o * o, axis=-1, keepdims=True) + RMS_EPS) * gain_all[:, cols]
        gr = g_ref[pl.ds(r, CHUNK), cols].astype(F32)
        o_ref[pl.ds(r, CHUNK), cols] = (y * (gr * _sigmoid(gr))).astype(o_ref.dtype)
        return st_new

    def chunk_body(ci, sts):
        r = pl.multiple_of(ci * CHUNK, CHUNK)
        return tuple(one_head(r, hh, sts[hh]) for hh in range(HGRN_HB))

    sts = lax.fori_loop(0, HGRN_ROWS // CHUNK, chunk_body, tuple(st_ref[hh] for hh in range(HGRN_HB)))
    for hh in range(HGRN_HB):
        st_ref[hh] = sts[hh]


def _hgrn(proj3, lb, gain):
    b, s, _ = proj3.shape
    wblk = HGRN_HB * HEAD_DIM
    nhb = N_HEADS // HGRN_HB

    def col(group):
        return pl.BlockSpec((None, HGRN_ROWS, wblk), lambda bi, h, si: (bi, si, group * nhb + h))

    vec = pl.BlockSpec((1, wblk), lambda bi, h, si: (0, h))
    return pl.pallas_call(
        _hgrn_kernel,
        grid=(b, nhb, s // HGRN_ROWS),
        in_specs=[col(0), col(1), col(2), col(3), vec, vec],
        out_specs=pl.BlockSpec((None, HGRN_ROWS, wblk), lambda bi, h, si: (bi, si, h)),
        out_shape=jax.ShapeDtypeStruct((b, s, GROUP_W), BF16),
        scratch_shapes=[
            pltpu.VMEM((HGRN_HB, HEAD_DIM, HEAD_DIM), F32),
            pltpu.VMEM((HGRN_ROWS, wblk), F32),
            pltpu.VMEM((HGRN_ROWS, wblk), F32),
            pltpu.VMEM((HGRN_ROWS, wblk), F32),
        ],
        compiler_params=_cparams(("parallel", "parallel", "arbitrary")),
        name="hgrn",
    )(proj3, proj3, proj3, proj3, lb.reshape(1, GROUP_W), gain.reshape(1, GROUP_W))


def _attn_kernel(q_ref, k0_ref, k1_ref, k2_ref, v0_ref, v1_ref, v2_ref, bias_ref, gain_ref, o_ref):
    i = pl.program_id(2)
    kpos = (i - 2) * ATT_Q + lax.broadcasted_iota(I32, (1, 3 * ATT_Q), 1)
    gain_all = gain_ref[...]
    for hh in range(ATT_HB):
        cols = slice(hh * HEAD_DIM, (hh + 1) * HEAD_DIM)
        q = q_ref[:, cols]
        k = jnp.concatenate([k0_ref[:, cols], k1_ref[:, cols], k2_ref[:, cols]], axis=0)
        v = jnp.concatenate([v0_ref[:, cols], v1_ref[:, cols], v2_ref[:, cols]], axis=0)
        s = lax.dot_general(q, k, (((1,), (1,)), ((), ())), preferred_element_type=F32)
        s = s * (HEAD_DIM ** -0.5) + bias_ref[hh]
        s = jnp.where(kpos >= 0, s, NEG)
        mx = jnp.max(s, axis=-1, keepdims=True)
        p = jnp.exp(s - mx)
        l = jnp.sum(p, axis=-1, keepdims=True)
        o = jnp.dot(p.astype(BF16), v, preferred_element_type=F32) / l
        y = o * lax.rsqrt(jnp.mean(o * o, axis=-1, keepdims=True) + RMS_EPS) * gain_all[:, cols]
        o_ref[:, cols] = y.astype(o_ref.dtype)


def _attn_bias_blocks(rel_bias):
    band = (LEFT_CHUNKS + 1) * CHUNK
    rel = jnp.arange(CHUNK)[:, None] + LEFT_CHUNKS * CHUNK - jnp.arange(band)[None, :]
    rel_idx = jnp.clip(rel, -(CHUNK - 1), REL_MAX) + (CHUNK - 1)
    bias = rel_bias.astype(F32)[:, rel_idx]
    n_q = ATT_Q // CHUNK
    rows = []
    for c in range(n_q):
        left = c * CHUNK
        right = 3 * ATT_Q - band - left
        rows.append(jnp.pad(bias, ((0, 0), (0, 0), (left, right)), constant_values=NEG))
    return jnp.concatenate(rows, axis=1)


def _attn(proj3, bias_blk, gain):
    b, s, _ = proj3.shape
    wblk = ATT_HB * HEAD_DIM
    nhb = N_HEADS // ATT_HB
    qc, kc, vc = 4 * nhb, 5 * nhb, 6 * nhb

    def kv(base, back):
        return pl.BlockSpec((None, ATT_Q, wblk), lambda h, bi, i: (bi, jnp.maximum(i - back, 0), base + h))

    return pl.pallas_call(
        _attn_kernel,
        grid=(nhb, b, s // ATT_Q),
        in_specs=[
            pl.BlockSpec((None, ATT_Q, wblk), lambda h, bi, i: (bi, i, qc + h)),
            kv(kc, 2), kv(kc, 1), kv(kc, 0),
            kv(vc, 2), kv(vc, 1), kv(vc, 0),
            pl.BlockSpec((ATT_HB, ATT_Q, 3 * ATT_Q), lambda h, bi, i: (h, 0, 0)),
            pl.BlockSpec((1, wblk), lambda h, bi, i: (0, h)),
        ],
        out_specs=pl.BlockSpec((None, ATT_Q, wblk), lambda h, bi, i: (bi, i, h)),
        out_shape=jax.ShapeDtypeStruct((b, s, GROUP_W), BF16),
        compiler_params=_cparams(("parallel", "parallel", "arbitrary")),
        name="attn",
    )(proj3, proj3, proj3, proj3, proj3, proj3, proj3, bias_blk, gain.reshape(1, GROUP_W))


def _out_proj_kernel(oh_ref, oa_ref, wh_ref, wa_ref, x_ref, h_ref):
    acc = jnp.dot(oh_ref[...], wh_ref[...], preferred_element_type=F32)
    acc = acc + jnp.dot(oa_ref[...], wa_ref[...], preferred_element_type=F32)
    h_ref[...] = x_ref[...] + acc


def _out_proj(oh, oa, wh, wa, x2):
    t, d = x2.shape
    return pl.pallas_call(
        _out_proj_kernel,
        grid=(t // TM_OUT, d // TN_OUT),
        in_specs=[
            pl.BlockSpec((TM_OUT, GROUP_W), lambda i, j: (i, 0)),
            pl.BlockSpec((TM_OUT, GROUP_W), lambda i, j: (i, 0)),
            pl.BlockSpec((GROUP_W, TN_OUT), lambda i, j: (0, j)),
            pl.BlockSpec((GROUP_W, TN_OUT), lambda i, j: (0, j)),
            pl.BlockSpec((TM_OUT, TN_OUT), lambda i, j: (i, j)),
        ],
        out_specs=pl.BlockSpec((TM_OUT, TN_OUT), lambda i, j: (i, j)),
        out_shape=jax.ShapeDtypeStruct((t, d), F32),
        compiler_params=_cparams(("parallel", "arbitrary")),
        name="out_proj",
    )(oh, oa, wh, wa, x2)


def _norm_router_kernel(h_ref, g_ref, wr_hi_ref, wr_lo_ref, br_ref, xp_ref, lg_ref):
    h = h_ref[...]
    ms = jnp.mean(h * h, axis=-1, keepdims=True)
    xn = h * lax.rsqrt(ms + RMS_EPS) * g_ref[...]
    xp_ref[...] = _pack_pairs(xn[:, :HALF], xn[:, HALF:])
    x_hi = xn.astype(BF16)
    x_lo = (xn - x_hi.astype(F32)).astype(BF16)
    lg = jnp.dot(x_hi, wr_hi_ref[...], preferred_element_type=F32)
    lg = lg + jnp.dot(x_lo, wr_hi_ref[...], preferred_element_type=F32)
    lg = lg + jnp.dot(x_hi, wr_lo_ref[...], preferred_element_type=F32)
    lg_ref[...] = lg + br_ref[...]


def _norm_router(h, gain, w_router, b_router):
    t, d = h.shape
    wr_hi = w_router.astype(BF16)
    wr_lo = (w_router - wr_hi.astype(F32)).astype(BF16)
    return pl.pallas_call(
        _norm_router_kernel,
        grid=(t // TM_NR,),
        in_specs=[
            pl.BlockSpec((TM_NR, d), lambda i: (i, 0)),
            pl.BlockSpec((1, d), lambda i: (0, 0)),
            pl.BlockSpec((d, N_EXPERTS), lambda i: (0, 0)),
            pl.BlockSpec((d, N_EXPERTS), lambda i: (0, 0)),
            pl.BlockSpec((1, N_EXPERTS), lambda i: (0, 0)),
        ],
        out_specs=[
            pl.BlockSpec((TM_NR, HALF), lambda i: (i, 0)),
            pl.BlockSpec((TM_NR, N_EXPERTS), lambda i: (i, 0)),
        ],
        out_shape=[
            jax.ShapeDtypeStruct((t, HALF), I32),
            jax.ShapeDtypeStruct((t, N_EXPERTS), F32),
        ],
        compiler_params=_cparams(("parallel",)),
        name="norm_router",
    )(h, gain.reshape(1, d), wr_hi, wr_lo, b_router.reshape(1, N_EXPERTS))


def _row_copy(src_hbm, dst_vmem, src_row, dst_row, sem):
    return pltpu.make_async_copy(src_hbm.at[pl.ds(src_row, 1)], dst_vmem.at[pl.ds(dst_row, 1)], sem)


def _gather_start(src_hbm, dst_vmem, ids_ref, n_rows, sem):
    def body(r, c):
        _row_copy(src_hbm, dst_vmem, ids_ref[0, r], r, sem).start()
        return c

    lax.fori_loop(0, n_rows, body, 0, unroll=DMA_UNROLL)


def _gather_wait(src_hbm, dst_vmem, n_rows, sem):
    def body(r, c):
        _row_copy(src_hbm, dst_vmem, 0, r, sem).wait()
        return c

    lax.fori_loop(0, n_rows, body, 0, unroll=DMA_UNROLL)


def _moe_kernel(te_ref, nu_ref, ids_ref, idn_ref, xp_hbm, wgl_ref, bgl_ref, wd_ref, bd_ref,
                y_ref, xg_ref, xb_ref, acc_ref, sem):
    t = pl.program_id(0)
    f = pl.program_id(1)
    n_used = nu_ref[0]
    valid = t < n_used
    slot = lax.rem(t, 2)

    @pl.when((t == 0) & (f == 0))
    def _():
        _gather_start(xp_hbm, xg_ref.at[0], ids_ref, TM_MOE, sem.at[0])

    @pl.when(valid & (f == 0))
    def _():
        _gather_wait(xp_hbm, xg_ref.at[slot], TM_MOE, sem.at[slot])
        xg = xg_ref[slot]
        xb_ref[:, :HALF] = _unpack_lo(xg).astype(BF16)
        xb_ref[:, HALF:] = _unpack_hi(xg).astype(BF16)

    @pl.when((f == 0) & (t + 1 < n_used))
    def _():
        _gather_start(xp_hbm, xg_ref.at[1 - slot], idn_ref, TM_MOE, sem.at[1 - slot])

    @pl.when(valid)
    def _():
        hgu = jnp.dot(xb_ref[...], wgl_ref[...], preferred_element_type=F32) + bgl_ref[...]
        half = MXU_COLS // 2
        acts = []
        for g in range(2 * TF_MOE // MXU_COLS):
            glu = jnp.minimum(hgu[:, g * MXU_COLS:g * MXU_COLS + half], SWIGLU_LIMIT)
            lin = jnp.clip(hgu[:, g * MXU_COLS + half:(g + 1) * MXU_COLS], -SWIGLU_LIMIT, SWIGLU_LIMIT)
            acts.append((glu * _sigmoid(SWIGLU_ALPHA * glu) * (lin + 1.0)).astype(BF16))
        act = jnp.concatenate(acts, axis=1)
        part = jnp.dot(act, wd_ref[...], preferred_element_type=F32)

        @pl.when(f == 0)
        def _():
            acc_ref[...] = part + bd_ref[...]

        @pl.when((f != 0) & (f != NF_MOE - 1))
        def _():
            acc_ref[...] += part

        @pl.when(f == NF_MOE - 1)
        def _():
            y_ref[...] = _pack_pairs(acc_ref[:, :HALF] + part[:, :HALF], acc_ref[:, HALF:] + part[:, HALF:])

    @pl.when(jnp.logical_not(valid) & (f == 0))
    def _():
        y_ref[...] = jnp.zeros_like(y_ref)


def _moe_experts(tile_expert, n_used, ids3, xp, wgl, bgl, wd, bd):
    n_max = ids3.shape[0]

    def fsel(t, f, nu):
        return jnp.where(t < nu[0], f, NF_MOE - 1)

    ids_spec = lambda shift: pl.BlockSpec(
        (None, 1, TM_MOE), lambda t, f, te, nu: (jnp.minimum(t + shift, n_max - 1), 0, 0),
        memory_space=pltpu.SMEM)
    grid_spec = pltpu.PrefetchScalarGridSpec(
        num_scalar_prefetch=2,
        grid=(n_max, NF_MOE),
        in_specs=[
            ids_spec(0),
            ids_spec(1),
            pl.BlockSpec(memory_space=pl.ANY),
            pl.BlockSpec((None, None, D_MODEL, 2 * TF_MOE), lambda t, f, te, nu: (te[t], fsel(t, f, nu), 0, 0)),
            pl.BlockSpec((None, None, 1, 2 * TF_MOE), lambda t, f, te, nu: (te[t], fsel(t, f, nu), 0, 0)),
            pl.BlockSpec((None, TF_MOE, D_MODEL), lambda t, f, te, nu: (te[t], fsel(t, f, nu), 0)),
            pl.BlockSpec((None, 1, D_MODEL), lambda t, f, te, nu: (te[t], 0, 0)),
        ],
        out_specs=pl.BlockSpec((TM_MOE, HALF), lambda t, f, te, nu: (t, 0)),
        scratch_shapes=[
            pltpu.VMEM((2, TM_MOE, HALF), I32),
            pltpu.VMEM((TM_MOE, D_MODEL), BF16),
            pltpu.VMEM((TM_MOE, D_MODEL), F32),
            pltpu.SemaphoreType.DMA((2,)),
        ],
    )
    return pl.pallas_call(
        _moe_kernel,
        grid_spec=grid_spec,
        out_shape=jax.ShapeDtypeStruct((n_max * TM_MOE, HALF), I32),
        compiler_params=_cparams(("arbitrary", "arbitrary"), VMEM_LIMIT_MOE),
        name="moe_experts",
    )(tile_expert, n_used, ids3, ids3, xp, wgl, bgl, wd, bd)


def _combine_kernel(pos_ref, posn_ref, gates_ref, h_ref, fg_ref, ys_hbm, o_ref, buf_ref, sem):
    i = pl.program_id(0)
    slot = lax.rem(i, 2)
    n_rows = TG * TOP_K

    @pl.when(i == 0)
    def _():
        _gather_start(ys_hbm, buf_ref.at[0], pos_ref, n_rows, sem.at[0])

    _gather_wait(ys_hbm, buf_ref.at[slot], n_rows, sem.at[slot])

    @pl.when(i + 1 < pl.num_programs(0))
    def _():
        _gather_start(ys_hbm, buf_ref.at[1 - slot], posn_ref, n_rows, sem.at[1 - slot])

    g = gates_ref[...]
    y_lo = h_ref[:, :HALF]
    y_hi = h_ref[:, HALF:]
    for k in range(TOP_K):
        p = buf_ref[slot, k * TG:(k + 1) * TG, :]
        y_lo = y_lo + _unpack_lo(p) * g[:, k:k + 1]
        y_hi = y_hi + _unpack_hi(p) * g[:, k:k + 1]
    ss = jnp.sum(y_lo * y_lo, axis=-1, keepdims=True) + jnp.sum(y_hi * y_hi, axis=-1, keepdims=True)
    rs = lax.rsqrt(ss * (1.0 / D_MODEL) + RMS_EPS)
    o_ref[:, :HALF] = y_lo * rs * fg_ref[:, :HALF]
    o_ref[:, HALF:] = y_hi * rs * fg_ref[:, HALF:]


def _combine(pos3, gates, h, final_gain, ys):
    t, d = h.shape
    n = t // TG
    pos_spec = lambda shift: pl.BlockSpec(
        (None, 1, TG * TOP_K), lambda i: (jnp.minimum(i + shift, n - 1), 0, 0), memory_space=pltpu.SMEM)
    return pl.pallas_call(
        _combine_kernel,
        grid=(n,),
        in_specs=[
            pos_spec(0),
            pos_spec(1),
            pl.BlockSpec((TG, TOP_K), lambda i: (i, 0)),
            pl.BlockSpec((TG, d), lambda i: (i, 0)),
            pl.BlockSpec((1, d), lambda i: (0, 0)),
            pl.BlockSpec(memory_space=pl.ANY),
        ],
        out_specs=pl.BlockSpec((TG, d), lambda i: (i, 0)),
        out_shape=jax.ShapeDtypeStruct((t, d), F32),
        scratch_shapes=[pltpu.VMEM((2, TG * TOP_K, HALF), I32), pltpu.SemaphoreType.DMA((2,))],
        compiler_params=_cparams(("arbitrary",)),
        name="combine",
    )(pos3, pos3, gates, h, final_gain.reshape(1, d), ys)


def _route(logits):
    t = logits.shape[0]
    a = t * TOP_K
    top_vals, top_idx = lax.top_k(logits, TOP_K)
    gates = jax.nn.softmax(top_vals, axis=-1)
    e_flat = top_idx.reshape(a).astype(I32)
    onehot = (e_flat[:, None] == jnp.arange(N_EXPERTS, dtype=I32)[None, :]).astype(I32)
    csum = jnp.cumsum(onehot, axis=0)
    rank = jnp.take_along_axis(csum, e_flat[:, None], axis=1)[:, 0] - 1
    counts = csum[-1]
    tiles_e = (counts + TM_MOE - 1) // TM_MOE
    tiles_end = jnp.cumsum(tiles_e)
    tiles_start = tiles_end - tiles_e
    dest = tiles_start[e_flat] * TM_MOE + rank
    n_max = a // TM_MOE + N_EXPERTS
    tok = jnp.arange(a, dtype=I32) // TOP_K
    row_tok = jnp.zeros((n_max * TM_MOE,), I32).at[dest].set(tok)
    n_used = tiles_end[-1].astype(I32)
    tile_ids = jnp.arange(n_max, dtype=I32)
    tile_expert = jnp.searchsorted(tiles_end, jnp.minimum(tile_ids, n_used - 1), side="right")
    tile_expert = jnp.minimum(tile_expert, N_EXPERTS - 1).astype(I32)
    return gates, dest.astype(I32), row_tok, tile_expert, n_used.reshape(1)


def kernel(x, norm_mix_gain, w_in, hgrn_lb_logits, hgrn_out_gain, rel_bias, attn_out_gain, w_out,
           norm_ffn_gain, w_router, b_router, w_gate_up, b_gate_up, w_down, b_down, final_gain):
    b, s, d = x.shape
    t = b * s
    x2 = x.reshape(t, d)

    w_in_b = w_in[0].astype(BF16)
    w_out_b = w_out[0].astype(BF16)
    wgl = _gate_up_prep(w_gate_up[0])
    bgl = _gate_up_bias_prep(b_gate_up[0])
    wd = w_down[0].astype(BF16)
    bd = b_down[0][:, None, :]
    lower_bound = jax.nn.softmax(hgrn_lb_logits.astype(F32), axis=0)[0]
    bias_blk = _attn_bias_blocks(rel_bias[0])

    proj = _in_proj(x2, norm_mix_gain[0], w_in_b)
    proj3 = proj.reshape(b, s, PROJ_W)
    o_h = _hgrn(proj3, lower_bound, hgrn_out_gain[0]).reshape(t, GROUP_W)
    o_a = _attn(proj3, bias_blk, attn_out_gain[0]).reshape(t, GROUP_W)
    h = _out_proj(o_h, o_a, w_out_b[:GROUP_W], w_out_b[GROUP_W:], x2)

    xp, logits = _norm_router(h, norm_ffn_gain[0], w_router[0], b_router[0])
    gates, dest, row_tok, tile_expert, n_used = _route(logits)
    ids3 = row_tok.reshape(-1, 1, TM_MOE)
    ys = _moe_experts(tile_expert, n_used, ids3, xp, wgl, bgl, wd, bd)

    pos3 = dest.reshape(t // TG, TG, TOP_K).transpose(0, 2, 1).reshape(t // TG, 1, TOP_K * TG)
    out = _combine(pos3, gates, h, final_gain, ys)
    return out.reshape(b, s, d)
```
